```python
import math
import jax
import jax.numpy as jnp
from jax import lax
import numpy as np

D_MODEL = 1024
BATCH = 8
SEQ = 2048
DEPTH = 4
DEC_BATCH = 128
DEC_SEQ = 8
PAST_LEN = 16384
PAGE_SIZE = 128

N_EVEN = (DEPTH + 1) // 2
N_ODD = DEPTH // 2
H_A = 4
DK_A = D_MODEL // 16
DV_A = D_MODEL // 8
GLA_RANK = 16
GLA_GATE_NORM = 16.0
GLA_CHUNK = 16
D_SSD = D_MODEL // 2
P_B = 64
H_B = D_SSD // P_B
N_B = 64
G_B = 2
SSD_CONV_DIM = D_SSD + 2 * G_B * N_B
SSD_CHUNK = 64
CONV_W = 4
H_C = 4
DK_C = D_MODEL // 16
DV_C = D_MODEL // 8
HGRN_CHUNK = 16
D_RG = D_MODEL // 2
H_D = 8
BD = D_RG // H_D
RG_C = 8.0
N_GROUPS = 4
E_PER_GROUP = 4
TOP_K_E = 2
D_EXPERT = D_MODEL // 2
DN_ALPHA = (2.0 * DEPTH) ** 0.25
DN_BETA = (8.0 * DEPTH) ** -0.25
EPS = 1e-5
_AB_SIZES = (H_A * DK_A, H_A * DK_A, H_A * DV_A, H_A * DV_A, GLA_RANK, D_SSD, SSD_CONV_DIM, H_B)
AB_IN = sum(_AB_SIZES)
AB_OUT = H_A * DV_A + D_SSD
_CD_SIZES = (H_C * DK_C, H_C * DK_C, H_C * DV_C, H_C * DV_C, D_RG, D_RG)
CD_IN = sum(_CD_SIZES)
CD_OUT = H_C * DV_C + D_RG

kernel_name = 'hybrid_gla_ssd_hgrn2_rglru_hmoe_step'

F32 = jnp.float32


def _split(u, sizes):
    idx = [int(s) for s in np.cumsum(sizes)[:-1]]
    return jnp.split(u, idx, axis=-1)


def _heads(t, h, d):
    b, l, _ = t.shape
    return t.reshape(b, l, h, d).transpose(0, 2, 1, 3)


def _rmsnorm(x, g):
    x32 = x.astype(F32)
    y = x32 * lax.rsqrt(jnp.mean(x32 * x32, axis=-1, keepdims=True) + EPS) * g
    return y.astype(x.dtype)


def _layernorm(x, g, b):
    x32 = x.astype(F32)
    mu = jnp.mean(x32, axis=-1, keepdims=True)
    xc = x32 - mu
    var = jnp.mean(xc * xc, axis=-1, keepdims=True)
    return (xc * lax.rsqrt(var + EPS) * g + b).astype(x.dtype)


def _causal_conv(x, buf, w, b):
    L = x.shape[1]
    W = w.shape[0]
    xp = jnp.concatenate([buf.astype(x.dtype), x], axis=1)
    y = b
    for j in range(W):
        y = y + xp[:, j:j + L] * w[j]
    return y, xp[:, L:]


def _chunk_recurrence(q_in, k_end, v, decay, S0):
    xs = tuple(jnp.moveaxis(t, 2, 0) for t in (q_in, k_end, v, decay))

    def step(S, inp):
        qc, kc, vc, dc = inp
        o = jnp.einsum('bhlk,bhkv->bhlv', qc, S)
        S = dc[..., None] * S + jnp.einsum('bhlk,bhlv->bhkv', kc, vc)
        return S, o

    S, o = lax.scan(step, S0, xs)
    return jnp.moveaxis(o, 0, 2), S


def _gla_chunked(q, k, v, log_f, S0, chunk):
    B, H, L, K = q.shape
    V = v.shape[-1]
    C = math.gcd(L, chunk)
    nc = L // C
    qc = q.astype(F32).reshape(B, H, nc, C, K)
    kc = k.astype(F32).reshape(B, H, nc, C, K)
    vc = v.astype(F32).reshape(B, H, nc, C, V)
    b = jnp.cumsum(log_f.astype(F32).reshape(B, H, nc, C, K), axis=3)
    mask = jnp.tril(jnp.ones((C, C), dtype=bool))
    diff = b[:, :, :, :, None, :] - b[:, :, :, None, :, :]
    decay = jnp.exp(jnp.where(mask[:, :, None], diff, -jnp.inf))
    scores = jnp.einsum('bhclk,bhcsk,bhclsk->bhcls', qc, kc, decay)
    o_intra = jnp.einsum('bhcls,bhcsv->bhclv', scores, vc)
    b_last = b[:, :, :, -1:, :]
    o_inter, S = _chunk_recurrence(qc * jnp.exp(b), kc * jnp.exp(b_last - b), vc,
                                   jnp.exp(b_last[:, :, :, 0, :]), S0.astype(F32))
    o = (o_intra + o_inter).reshape(B, H, L, V).astype(v.dtype)
    return o, S.astype(S0.dtype)


def _ssd_chunked(x, dt, A, Bm, Cm, h0, chunk):
    B, L, H, P = x.shape
    G = Bm.shape[2]
    C = math.gcd(L, chunk)
    nc = L // C

    def blk(t):
        return jnp.moveaxis(t, 2, 1).reshape((B, H, nc, C) + t.shape[3:])

    dt32 = dt.astype(F32)
    a = blk(dt32 * A.astype(F32))
    xc = blk(x.astype(F32) * dt32[..., None])
    Bc = blk(jnp.repeat(Bm.astype(F32), H // G, axis=2))
    Cc = blk(jnp.repeat(Cm.astype(F32), H // G, axis=2))
    cs = jnp.cumsum(a, axis=-1)
    mask = jnp.tril(jnp.ones((C, C), dtype=bool))
    seg = cs[..., :, None] - cs[..., None, :]
    Lmat = jnp.exp(jnp.where(mask, seg, -jnp.inf))
    scores = jnp.einsum('bhcln,bhcsn->bhcls', Cc, Bc) * Lmat
    y_intra = jnp.einsum('bhcls,bhcsp->bhclp', scores, xc)
    cs_last = cs[..., -1:]
    y_inter, h = _chunk_recurrence(Cc * jnp.exp(cs)[..., None], Bc * jnp.exp(cs_last - cs)[..., None], xc,
                                   jnp.exp(cs_last), h0.astype(F32))
    y = (y_intra + y_inter).reshape(B, H, L, P).transpose(0, 2, 1, 3)
    return y.astype(x.dtype), h.astype(h0.dtype)


def _gla_mixer(xq, xk, xv, xg, xlr, w_gk2, b_gk, norm_g, S0):
    B, L, _ = xq.shape
    q = _heads(xq, H_A, DK_A) * (DK_A ** -0.5)
    k = _heads(xk, H_A, DK_A)
    v = _heads(xv, H_A, DV_A)
    log_f = jax.nn.log_sigmoid((xlr @ w_gk2 + b_gk).astype(F32)) / GLA_GATE_NORM
    o, S = _gla_chunked(q, k, v, _heads(log_f, H_A, DK_A), S0, GLA_CHUNK)
    o = _rmsnorm(o.transpose(0, 2, 1, 3), norm_g) * jax.nn.silu(xg.reshape(B, L, H_A, DV_A))
    return o.reshape(B, L, H_A * DV_A), S


def _ssd_mixer(xz, xbc, xdt, conv_buf, conv_w, conv_b, dt_bias, a_log, d_skip, norm_g, h0):
    B, L, _ = xz.shape
    xbc, new_buf = _causal_conv(xbc, conv_buf, conv_w, conv_b)
    xbc = jax.nn.silu(xbc)
    xs, Bm, Cm = _split(xbc, (D_SSD, G_B * N_B, G_B * N_B))
    xs = xs.reshape(B, L, H_B, P_B)
    Bm = Bm.reshape(B, L, G_B, N_B)
    Cm = Cm.reshape(B, L, G_B, N_B)
    dt = jax.nn.softplus((xdt + dt_bias).astype(F32))
    A = -jnp.exp(a_log.astype(F32))
    y, h = _ssd_chunked(xs, dt, A, Bm, Cm, h0, SSD_CHUNK)
    y = y + d_skip[:, None] * xs
    y = y.reshape(B, L, D_SSD) * jax.nn.silu(xz)
    y = _rmsnorm(y.reshape(B, L, G_B, D_SSD // G_B), norm_g.reshape(G_B, D_SSD // G_B)).reshape(B, L, D_SSD)
    return y, new_buf, h


def _hgrn_mixer(xq, xf, xi, xg, lb, norm_g, S0):
    B, L, _ = xq.shape
    q = _heads(jax.nn.silu(xq), H_C, DK_C)
    log_f = jnp.logaddexp(jnp.log(lb), jnp.log1p(-lb) + jax.nn.log_sigmoid(xf.astype(F32)))
    k = -jnp.expm1(log_f)
    v = _heads(xi, H_C, DV_C)
    o, S = _gla_chunked(q, _heads(k, H_C, DK_C), v, _heads(log_f, H_C, DK_C), S0, HGRN_CHUNK)
    o = _rmsnorm(o.transpose(0, 2, 1, 3), norm_g) * jax.nn.silu(xg.reshape(B, L, H_C, DV_C))
    return o.reshape(B, L, H_C * DV_C), S


def _lin_combine(e1, e2):
    a1, b1 = e1
    a2, b2 = e2
    return (a1 * a2, a2 * b1 + b2)


def _rglru_mixer(xr, xgate, conv_buf, conv_w, conv_b, w_a, b_a, w_i, b_i, lam, h0):
    B, L, _ = xr.shape
    xc, new_buf = _causal_conv(xr, conv_buf, conv_w, conv_b)
    xb = xc.reshape(B, L, H_D, BD)
    r = jax.nn.sigmoid((jnp.einsum('blhi,hij->blhj', xb, w_a).reshape(B, L, D_RG) + b_a).astype(F32))
    ig = jax.nn.sigmoid((jnp.einsum('blhi,hij->blhj', xb, w_i).reshape(B, L, D_RG) + b_i).astype(F32))
    log_a = -RG_C * r * jax.nn.softplus(-lam.astype(F32))
    a = jnp.exp(log_a)
    u = jnp.sqrt(-jnp.expm1(2.0 * log_a)) * (ig * xc.astype(F32))
    a_cum, u_cum = lax.associative_scan(_lin_combine, (a, u), axis=1)
    h = a_cum * h0.astype(F32)[:, None, :] + u_cum
    y = (h * jax.nn.gelu(xgate.astype(F32))).astype(xr.dtype)
    return y, new_buf, h[:, -1].astype(h0.dtype)


def _hmoe(x, w_group, b_group, w_router, b_router, w_gate, w_up, w_down):
    B, L, D = x.shape
    t = x.reshape(B * L, D)
    g_logits = (t @ w_group + b_group).astype(F32)
    g_prob = jax.nn.softmax(g_logits, axis=-1)
    _, g_idx = lax.top_k(g_logits, 1)
    p_g = jnp.take_along_axis(g_prob, g_idx, axis=-1)
    e_all = (jnp.einsum('td,gde->tge', t, w_router) + b_router).astype(F32)
    e_logits = jnp.take_along_axis(e_all, g_idx[:, :, None], axis=1)[:, 0]
    top_v, top_i = lax.top_k(e_logits, TOP_K_E)
    w_k = jax.nn.softmax(top_v, axis=-1)
    within = jnp.einsum('tk,tke->te', w_k, jax.nn.one_hot(top_i, E_PER_GROUP, dtype=F32))
    combine = (p_g[:, :, None] * jax.nn.one_hot(g_idx[:, 0], N_GROUPS, dtype=F32)[:, :, None]
               * within[:, None, :]).astype(x.dtype)
    h = jax.nn.silu(jnp.einsum('td,gedf->tgef', t, w_gate)) * jnp.einsum('td,gedf->tgef', t, w_up)
    out = jnp.einsum('tgef,tge,gefd->td', h, combine, w_down)
    return out.reshape(B, L, D)


def _trunk(x, st_gla, st_ssd, st_ssd_conv, st_hgrn, st_rg, st_rg_conv, p):
    lb_all = jnp.cumsum(jax.nn.softmax(p['hgrn_lb_logits'].astype(F32), axis=0), axis=0)
    lb_all = lb_all - lb_all[:1]
    gla_n, ssd_n, ssdc_n, hgrn_n, rg_n, rgc_n = [], [], [], [], [], []
    for l in range(DEPTH):
        i = l // 2
        if l % 2 == 0:
            q_a, k_a, v_a, g_a, lr_a, z_b, xbc_b, dt_b = _split(x @ p['w_in_ab'][i], _AB_SIZES)
            o_a, s_a = _gla_mixer(q_a, k_a, v_a, g_a, lr_a, p['gla_w_gk2'][i], p['gla_b_gk'][i],
                                  p['gla_norm_g'][i], st_gla[i])
            o_b, c_b, s_b = _ssd_mixer(z_b, xbc_b, dt_b, st_ssd_conv[i], p['ssd_conv_w'][i], p['ssd_conv_b'][i],
                                       p['ssd_dt_bias'][i], p['ssd_a_log'][i], p['ssd_d'][i],
                                       p['ssd_norm_g'][i], st_ssd[i])
            mix = jnp.concatenate([o_a, o_b], axis=-1) @ p['w_out_ab'][i]
            gla_n.append(s_a)
            ssd_n.append(s_b)
            ssdc_n.append(c_b)
        else:
            q_c, f_c, i_c, g_c, x_d, gate_d = _split(x @ p['w_in_cd'][i], _CD_SIZES)
            o_c, s_c = _hgrn_mixer(q_c, f_c, i_c, g_c, lb_all[i], p['hgrn_norm_g'][i], st_hgrn[i])
            o_d, c_d, s_d = _rglru_mixer(x_d, gate_d, st_rg_conv[i], p['rg_conv_w'][i], p['rg_conv_b'][i],
                                         p['rg_w_a'][i], p['rg_b_a'][i], p['rg_w_i'][i], p['rg_b_i'][i],
                                         p['rg_lambda'][i], st_rg[i])
            mix = jnp.concatenate([o_c, o_d], axis=-1) @ p['w_out_cd'][i]
            hgrn_n.append(s_c)
            rg_n.append(s_d)
            rgc_n.append(c_d)
        x = _layernorm(DN_ALPHA * x + mix, p['ln_mix_g'][l], p['ln_mix_b'][l])
        ffn = _hmoe(x, p['moe_w_group'][l], p['moe_b_group'][l], p['moe_w_router'][l], p['moe_b_router'][l],
                    p['moe_w_gate'][l], p['moe_w_up'][l], p['moe_w_down'][l])
        x = _layernorm(DN_ALPHA * x + ffn, p['ln_ffn_g'][l], p['ln_ffn_b'][l])
    return (x, jnp.stack(gla_n), jnp.stack(ssd_n), jnp.stack(ssdc_n),
            jnp.stack(hgrn_n), jnp.stack(rg_n), jnp.stack(rgc_n))


def setup_inputs(seed: int = 0) -> dict:
    key = jax.random.key(seed)
    ks = iter(jax.random.split(key, 64))

    def nrm(shape, scale):
        return scale * jax.random.normal(next(ks), shape, jnp.float32)

    def gain(shape):
        return 1.0 + nrm(shape, 0.1)

    NE, NO = N_EVEN, N_ODD
    dt0 = jnp.exp(jax.random.uniform(next(ks), (NE, H_B), jnp.float32, math.log(1e-3), math.log(1e-1)))
    a0 = jax.random.uniform(next(ks), (NO, D_RG), jnp.float32, 0.9, 0.999)
    s0 = a0 ** (1.0 / RG_C)
    return {
        'x_prompt': nrm((BATCH, SEQ, D_MODEL), 1.0),
        'x_sample': nrm((DEC_BATCH, DEC_SEQ, D_MODEL), 1.0),
        'state_gla': nrm((NE, DEC_BATCH, H_A, DK_A, DV_A), 0.5),
        'state_ssd': nrm((NE, DEC_BATCH, H_B, N_B, P_B), 0.5),
        'state_ssd_conv': nrm((NE, DEC_BATCH, CONV_W - 1, SSD_CONV_DIM), 1.0),
        'state_hgrn': nrm((NO, DEC_BATCH, H_C, DK_C, DV_C), 0.5),
        'state_rglru': nrm((NO, DEC_BATCH, D_RG), 0.5),
        'state_rglru_conv': nrm((NO, DEC_BATCH, CONV_W - 1, D_RG), 1.0),
        'w_in_ab': nrm((NE, D_MODEL, AB_IN), D_MODEL ** -0.5),
        'gla_w_gk2': nrm((NE, GLA_RANK, H_A * DK_A), GLA_RANK ** -0.5),
        'gla_b_gk': nrm((NE, H_A * DK_A), 0.1),
        'gla_norm_g': gain((NE, DV_A)),
        'ssd_conv_w': nrm((NE, CONV_W, SSD_CONV_DIM), CONV_W ** -0.5),
        'ssd_conv_b': nrm((NE, SSD_CONV_DIM), 0.02),
        'ssd_dt_bias': dt0 + jnp.log(-jnp.expm1(-dt0)),
        'ssd_a_log': jnp.log(jax.random.uniform(next(ks), (NE, H_B), jnp.float32, 1.0, 16.0)),
        'ssd_d': gain((NE, H_B)),
        'ssd_norm_g': gain((NE, D_SSD)),
        'w_out_ab': nrm((NE, AB_OUT, D_MODEL), AB_OUT ** -0.5 * DN_BETA),
        'w_in_cd': nrm((NO, D_MODEL, CD_IN), D_MODEL ** -0.5),
        'hgrn_lb_logits': nrm((NO, H_C * DK_C), 0.5),
        'hgrn_norm_g': gain((NO, DV_C)),
        'rg_conv_w': nrm((NO, CONV_W, D_RG), CONV_W ** -0.5),
        'rg_conv_b': nrm((NO, D_RG), 0.02),
        'rg_w_a': nrm((NO, H_D, BD, BD), BD ** -0.5),
        'rg_b_a': nrm((NO, D_RG), 0.1),
        'rg_w_i': nrm((NO, H_D, BD, BD), BD ** -0.5),
        'rg_b_i': nrm((NO, D_RG), 0.1),
        'rg_lambda': jnp.log(s0) - jnp.log1p(-s0),
        'w_out_cd': nrm((NO, CD_OUT, D_MODEL), CD_OUT ** -0.5 * DN_BETA),
        'ln_mix_g': gain((DEPTH, D_MODEL)),
        'ln_mix_b': nrm((DEPTH, D_MODEL), 0.02),
        'ln_ffn_g': gain((DEPTH, D_MODEL)),
        'ln_ffn_b': nrm((DEPTH, D_MODEL), 0.02),
        'moe_w_group': nrm((DEPTH, D_MODEL, N_GROUPS), D_MODEL ** -0.5),
        'moe_b_group': nrm((DEPTH, N_GROUPS), 0.01),
        'moe_w_router': nrm((DEPTH, N_GROUPS, D_MODEL, E_PER_GROUP), D_MODEL ** -0.5),
        'moe_b_router': nrm((DEPTH, N_GROUPS, E_PER_GROUP), 0.01),
        'moe_w_gate': nrm((DEPTH, N_GROUPS, E_PER_GROUP, D_MODEL, D_EXPERT), D_MODEL ** -0.5),
        'moe_w_up': nrm((DEPTH, N_GROUPS, E_PER_GROUP, D_MODEL, D_EXPERT), D_MODEL ** -0.5),
        'moe_w_down': nrm((DEPTH, N_GROUPS, E_PER_GROUP, D_EXPERT, D_MODEL), D_EXPERT ** -0.5 * DN_BETA),
    }


def reference(x_prompt, x_sample, state_gla, state_ssd, state_ssd_conv, state_hgrn, state_rglru, state_rglru_conv,
              w_in_ab, gla_w_gk2, gla_b_gk, gla_norm_g, ssd_conv_w, ssd_conv_b, ssd_dt_bias, ssd_a_log, ssd_d,
              ssd_norm_g, w_out_ab, w_in_cd, hgrn_lb_logits, hgrn_norm_g, rg_conv_w, rg_conv_b, rg_w_a, rg_b_a,
              rg_w_i, rg_b_i, rg_lambda, w_out_cd, ln_mix_g, ln_mix_b, ln_ffn_g, ln_ffn_b, moe_w_group,
              moe_b_group, moe_w_router, moe_b_router, moe_w_gate, moe_w_up, moe_w_down):
    p = dict(w_in_ab=w_in_ab, gla_w_gk2=gla_w_gk2, gla_b_gk=gla_b_gk, gla_norm_g=gla_norm_g,
             ssd_conv_w=ssd_conv_w, ssd_conv_b=ssd_conv_b, ssd_dt_bias=ssd_dt_bias, ssd_a_log=ssd_a_log,
             ssd_d=ssd_d, ssd_norm_g=ssd_norm_g, w_out_ab=w_out_ab, w_in_cd=w_in_cd,
             hgrn_lb_logits=hgrn_lb_logits, hgrn_norm_g=hgrn_norm_g, rg_conv_w=rg_conv_w, rg_conv_b=rg_conv_b,
             rg_w_a=rg_w_a, rg_b_a=rg_b_a, rg_w_i=rg_w_i, rg_b_i=rg_b_i, rg_lambda=rg_lambda, w_out_cd=w_out_cd,
             ln_mix_g=ln_mix_g, ln_mix_b=ln_mix_b, ln_ffn_g=ln_ffn_g, ln_ffn_b=ln_ffn_b,
             moe_w_group=moe_w_group, moe_b_group=moe_b_group, moe_w_router=moe_w_router,
             moe_b_router=moe_b_router, moe_w_gate=moe_w_gate, moe_w_up=moe_w_up, moe_w_down=moe_w_down)
    nb = x_prompt.shape[0]

    def fresh(s):
        return jnp.zeros((s.shape[0], nb) + s.shape[2:], x_prompt.dtype)

    y_prompt, gla_p, ssd_p, ssdc_p, hgrn_p, rg_p, rgc_p = _trunk(
        x_prompt, fresh(state_gla), fresh(state_ssd), fresh(state_ssd_conv), fresh(state_hgrn),
        fresh(state_rglru), fresh(state_rglru_conv), p)
    y_sample, gla_s, ssd_s, ssdc_s, hgrn_s, rg_s, rgc_s = _trunk(
        x_sample, state_gla, state_ssd, state_ssd_conv, state_hgrn, state_rglru, state_rglru_conv, p)
    return (y_prompt, y_sample, gla_p, ssd_p, ssdc_p, hgrn_p, rg_p, rgc_p,
            gla_s, ssd_s, ssdc_s, hgrn_s, rg_s, rgc_s)
```

```python
import functools
import math

import jax
import jax.numpy as jnp
import numpy as np
from jax import lax
from jax.experimental import pallas as pl
from jax.experimental.pallas import tpu as pltpu

F32 = jnp.float32
BF16 = jnp.bfloat16

V7X_LANES = 128
V7X_SUBLANES = 8
V7X_VMEM_BYTES = 64 * 1024 * 1024
VMEM_LIMIT_BYTES = V7X_VMEM_BYTES - 8 * 1024 * 1024

D_MODEL = 1024
DEPTH = 4
N_HEADS_LIN = 4
DK_LIN = 64
DV_LIN = 128
GLA_RANK = 16
GLA_GATE_NORM = 16.0
D_SSD = 512
P_SSD = 64
H_SSD = 8
N_SSD = 64
G_SSD = 2
SSD_CONV_DIM = D_SSD + 2 * G_SSD * N_SSD
CONV_W = 4
D_RG = 512
H_RG = 8
RG_C = 8.0
N_GROUPS = 4
E_PER_GROUP = 4
N_EXPERTS = N_GROUPS * E_PER_GROUP
D_EXPERT = 512
DN_ALPHA = (2.0 * DEPTH) ** 0.25
EPS = 1e-5

ROW_BLOCK = 128
DT_LANE0 = GLA_RANK
CONV_PAD = V7X_SUBLANES


def _cparams(sem, vmem=VMEM_LIMIT_BYTES):
    return pltpu.CompilerParams(dimension_semantics=sem, vmem_limit_bytes=vmem)


def _row_tile(t, cap):
    tm = ROW_BLOCK
    while tm * 2 <= cap and t % (tm * 2) == 0:
        tm *= 2
    return tm


def _sigmoid(x):
    return 1.0 / (1.0 + jnp.exp(-x))


def _silu(x):
    return x * _sigmoid(x)


def _softplus(x):
    return jnp.maximum(x, 0.0) + jnp.log1p(jnp.exp(-jnp.abs(x)))


def _log_sigmoid(x):
    return -_softplus(-x)


def _expm1(x):
    t = jnp.tanh(0.5 * x)
    return 2.0 * t / (1.0 - t)


def _gelu_tanh(x):
    c = math.sqrt(2.0 / math.pi)
    return 0.5 * x * (1.0 + jnp.tanh(c * (x + 0.044715 * (x * x * x))))


def _dot(a, b):
    return jnp.dot(a, b, preferred_element_type=F32)


def _dot_nt(a, b):
    return lax.dot_general(a, b, (((1,), (1,)), ((), ())), preferred_element_type=F32)


def _dot_tn(a, b):
    return lax.dot_general(a, b, (((0,), (0,)), ((), ())), preferred_element_type=F32)


def _split3(x):
    hi = x.astype(BF16)
    r1 = x - hi.astype(F32)
    mid = r1.astype(BF16)
    lo = (r1 - mid.astype(F32)).astype(BF16)
    return hi, mid, lo


def _sel_rows(sel, x):
    w = x.shape[1]
    y = _dot(sel, jnp.concatenate(_split3(x), axis=1))
    return y[:, :w] + y[:, w:2 * w] + y[:, 2 * w:]


def _sel_lanes(x, sel):
    r = x.shape[0]
    y = _dot(jnp.concatenate(_split3(x), axis=0), sel)
    return y[:r] + y[r:2 * r] + y[2 * r:]


def _layernorm(x, g, b):
    mu = jnp.mean(x, axis=-1, keepdims=True)
    xc = x - mu
    var = jnp.mean(xc * xc, axis=-1, keepdims=True)
    return xc * lax.rsqrt(var + EPS) * g + b


def _rmsnorm(x, g):
    return x * lax.rsqrt(jnp.mean(x * x, axis=-1, keepdims=True) + EPS) * g


def _seq_consts(rows, lseq):
    l = np.arange(rows)[:, None]
    m = np.arange(rows)[None, :]
    same = (l // lseq) == (m // lseq)
    mats = [(m <= l) & same, same]
    for n in range(int(math.log2(lseq))):
        later = ((l >> n) & 1) == 1
        mid_q = (l >> n) << n
        mid_k = ((l >> n) + 1) << n
        mats.append((later & (m >= mid_q) & (m <= l)) | (~later & (m > l) & (m < mid_k)))
    return jnp.asarray(np.concatenate(mats, axis=0).astype(np.float32), dtype=BF16)


def _head_expand_const():
    e = np.zeros((V7X_LANES, D_SSD), np.float32)
    for h in range(H_SSD):
        e[DT_LANE0 + h, h * P_SSD:(h + 1) * P_SSD] = 1.0
    return jnp.asarray(e, dtype=BF16)


def _inproj_kernel(widths, x_ref, w_ref, *o_refs):
    xb = x_ref[...].astype(BF16)
    off = 0
    for o_ref, wd in zip(o_refs, widths):
        o_ref[...] = _dot(xb, w_ref[:, off:off + wd])
        off += wd


def _inproj(x, w, widths):
    t = x.shape[0]
    tm = _row_tile(t, 512)
    return pl.pallas_call(
        functools.partial(_inproj_kernel, widths),
        grid=(t // tm,),
        in_specs=[pl.BlockSpec((tm, D_MODEL), lambda i: (i, 0)),
                  pl.BlockSpec(w.shape, lambda i: (0, 0))],
        out_specs=[pl.BlockSpec((tm, wd), lambda i: (i, 0)) for wd in widths],
        out_shape=[jax.ShapeDtypeStruct((t, wd), F32) for wd in widths],
        compiler_params=_cparams(("parallel",)),
        name="inproj",
    )(x, w)


def _outproj_ln_kernel(x_ref, oa_ref, ob_ref, w_ref, g_ref, b_ref, o_ref):
    half = oa_ref.shape[1]
    mix = _dot(oa_ref[...].astype(BF16), w_ref[:half, :]) + _dot(ob_ref[...].astype(BF16), w_ref[half:, :])
    o_ref[...] = _layernorm(DN_ALPHA * x_ref[...] + mix, g_ref[...], b_ref[...])


def _outproj_ln(x, oa, ob, w, g, b):
    t = x.shape[0]
    tm = _row_tile(t, 512)
    row = lambda i: (i, 0)
    fixed = lambda i: (0, 0)
    return pl.pallas_call(
        _outproj_ln_kernel,
        grid=(t // tm,),
        in_specs=[pl.BlockSpec((tm, D_MODEL), row), pl.BlockSpec((tm, oa.shape[1]), row),
                  pl.BlockSpec((tm, ob.shape[1]), row), pl.BlockSpec(w.shape, fixed),
                  pl.BlockSpec((1, D_MODEL), fixed), pl.BlockSpec((1, D_MODEL), fixed)],
        out_specs=pl.BlockSpec((tm, D_MODEL), row),
        out_shape=jax.ShapeDtypeStruct((t, D_MODEL), F32),
        compiler_params=_cparams(("parallel",)),
        name="outproj_ln",
    )(x, oa, ob, w, g, b)


def _lin_attn_kernel(mode, nseq, lseq, has_state, *refs):
    rows = nseq * lseq
    nlev = int(math.log2(lseq))
    if mode == "gla":
        q_ref, k_ref, v_ref, g_ref, lr_ref, wgk_ref, bgk_ref, ng_ref, sel_ref = refs[:9]
        rest = refs[9:]
    else:
        q_ref, k_ref, v_ref, g_ref, lb_ref, ng_ref, sel_ref = refs[:7]
        rest = refs[7:]
    if has_state:
        s0_ref, o_ref, sn_ref, st_ref = rest
    else:
        o_ref, sn_ref, st_ref = rest
    step = pl.program_id(1)

    @pl.when(step == 0)
    def _load_state():
        for s in range(nseq):
            for h in range(N_HEADS_LIN):
                if has_state:
                    padded = jnp.concatenate([s0_ref[s, h], jnp.zeros((V7X_LANES - DK_LIN, DV_LIN), F32)], axis=0)
                    st_ref[s * N_HEADS_LIN + h] = padded.T
                else:
                    st_ref[s * N_HEADS_LIN + h] = jnp.zeros((DV_LIN, V7X_LANES), F32)

    lane = lax.broadcasted_iota(jnp.int32, (1, N_HEADS_LIN * V7X_LANES), 1)
    real = (lane & (V7X_LANES - 1)) < DK_LIN
    if mode == "gla":
        q_all = q_ref[...] * (DK_LIN ** -0.5)
        k_all = k_ref[...]
        gate = _dot(lr_ref[...].astype(BF16), wgk_ref[...]) + bgk_ref[...]
        lf_all = _log_sigmoid(gate) / GLA_GATE_NORM
    else:
        q_all = _silu(q_ref[...])
        lb = lb_ref[...]
        a1 = jnp.log(lb)
        a2 = jnp.log1p(-lb) + _log_sigmoid(k_ref[...])
        lf_all = jnp.maximum(a1, a2) + jnp.log1p(jnp.exp(-jnp.abs(a1 - a2)))
        k_all = jnp.where(real, -_expm1(lf_all), 0.0)

    row = lax.broadcasted_iota(jnp.int32, (rows, rows), 0)
    col = lax.broadcasted_iota(jnp.int32, (rows, rows), 1)
    diff_bits = jnp.where(row > col, row ^ col, 0)
    eye = row == col
    sel = sel_ref[...]

    for h in range(N_HEADS_LIN):
        hs = slice(h * V7X_LANES, (h + 1) * V7X_LANES)
        qh, kh, lfh = q_all[:, hs], k_all[:, hs], lf_all[:, hs]
        vb = v_ref[:, hs].astype(BF16)
        sums = _sel_rows(sel, lfh)
        b_inc = sums[:rows]
        b_tot = sums[rows:2 * rows]
        scores = jnp.where(eye, _dot_nt(qh.astype(BF16), kh.astype(BF16)), 0.0)
        for n in range(nlev):
            a = jnp.exp(sums[(2 + n) * rows:(3 + n) * rows])
            part = _dot_nt((qh * a).astype(BF16), (kh * a).astype(BF16))
            scores = scores + jnp.where((diff_bits >> n) == 1, part, 0.0)
        o = _dot(scores.astype(BF16), vb)
        q_dec = qh * jnp.exp(b_inc)
        k_dec = kh * jnp.exp(b_tot - b_inc)
        carry = []
        for s in range(nseq):
            rs = slice(s * lseq, (s + 1) * lseq)
            idx = s * N_HEADS_LIN + h
            state = st_ref[idx]
            carry.append(_dot_nt(q_dec[rs].astype(BF16), state.astype(BF16)))
            upd = _dot_tn(v_ref[rs, hs].astype(BF16), k_dec[rs].astype(BF16))
            st_ref[idx] = state * jnp.exp(b_tot[s * lseq:s * lseq + 1]) + upd
        o = o + (carry[0] if nseq == 1 else jnp.concatenate(carry, axis=0))
        vs = slice(h * DV_LIN, (h + 1) * DV_LIN)
        o_ref[:, vs] = _rmsnorm(o, ng_ref[...]) * _silu(g_ref[:, vs])

    @pl.when(step == pl.num_programs(1) - 1)
    def _store_state():
        for s in range(nseq):
            for h in range(N_HEADS_LIN):
                sn_ref[s, h] = st_ref[s * N_HEADS_LIN + h].T[:DK_LIN, :]


def _lin_attn(mode, acts, params, state, nb, seqlen, row0):
    lseq = min(seqlen, ROW_BLOCK)
    nseq = ROW_BLOCK // lseq
    nblk_b, nblk_l = nb // nseq, seqlen // lseq
    blk0 = row0 // ROW_BLOCK
    rowmap = lambda b, l: (blk0 + b * nblk_l + l, 0)
    fixed = lambda b, l: (0, 0)
    sel = _seq_consts(ROW_BLOCK, lseq)
    in_specs = [pl.BlockSpec((ROW_BLOCK, a.shape[1]), rowmap) for a in acts]
    in_specs += [pl.BlockSpec(p.shape, fixed) for p in params]
    in_specs.append(pl.BlockSpec(sel.shape, fixed))
    args = list(acts) + list(params) + [sel]
    st_block = (nseq, N_HEADS_LIN, DK_LIN, DV_LIN)
    if state is not None:
        in_specs.append(pl.BlockSpec(st_block, lambda b, l: (b, 0, 0, 0)))
        args.append(state)
    return pl.pallas_call(
        functools.partial(_lin_attn_kernel, mode, nseq, lseq, state is not None),
        grid=(nblk_b, nblk_l),
        in_specs=in_specs,
        out_specs=[pl.BlockSpec((ROW_BLOCK, N_HEADS_LIN * DV_LIN), lambda b, l: (b * nblk_l + l, 0)),
                   pl.BlockSpec(st_block, lambda b, l: (b, 0, 0, 0))],
        out_shape=[jax.ShapeDtypeStruct((nb * seqlen, N_HEADS_LIN * DV_LIN), F32),
                   jax.ShapeDtypeStruct((nb,) + st_block[1:], F32)],
        scratch_shapes=[pltpu.VMEM((nseq * N_HEADS_LIN, DV_LIN, V7X_LANES), F32)],
        compiler_params=_cparams(("parallel", "arbitrary")),
        name="lin_attn_" + mode,
    )(*args)


def _causal_conv(x_ref, hist_ref, w_ref, b_ref, cbuf_ref, cnew_ref, xc_ref, nseq, lseq, first, last):
    keep = CONV_W - 1
    chans = x_ref.shape[1]

    @pl.when(first)
    def _init():
        for s in range(nseq):
            if cbuf_ref is None:
                hist_ref[s, 0:CONV_PAD, :] = jnp.zeros((CONV_PAD, chans), F32)
            else:
                hist_ref[s, CONV_PAD - keep:CONV_PAD, :] = cbuf_ref[s]

    for s in range(nseq):
        hist_ref[s, CONV_PAD:CONV_PAD + lseq, :] = x_ref[s * lseq:(s + 1) * lseq, :]
        y = b_ref[...]
        for j in range(CONV_W):
            start = CONV_PAD - keep + j
            y = y + hist_ref[s, start:start + lseq, :] * w_ref[j:j + 1, :]
        xc_ref[s * lseq:(s + 1) * lseq, :] = y
        tail = hist_ref[s, CONV_PAD + lseq - keep:CONV_PAD + lseq, :]
        hist_ref[s, CONV_PAD - keep:CONV_PAD, :] = tail

    @pl.when(last)
    def _emit():
        for s in range(nseq):
            cnew_ref[s] = hist_ref[s, CONV_PAD - keep:CONV_PAD, :]


def _ssd_kernel(nseq, lseq, has_state, *refs):
    rows = nseq * lseq
    (z_ref, xbc_ref, dt_ref, cw_ref, cb_ref, dtb_ref, alog_ref, dexp_ref, ng_ref, exp_ref,
     sel_ref) = refs[:11]
    rest = refs[11:]
    if has_state:
        cbuf_ref, h0_ref, y_ref, cnew_ref, hn_ref, hist_ref, xc_ref, hs_ref = rest
    else:
        y_ref, cnew_ref, hn_ref, hist_ref, xc_ref, hs_ref = rest
        cbuf_ref = h0_ref = None
    step = pl.program_id(1)
    first = step == 0
    last = step == pl.num_programs(1) - 1
    pairs = H_SSD // 2
    pairs_per_group = pairs // G_SSD

    @pl.when(first)
    def _load_state():
        for s in range(nseq):
            for j in range(pairs):
                hs_ref[s * pairs + j] = jnp.zeros((V7X_LANES, V7X_LANES), F32)
                if has_state:
                    g = j // pairs_per_group
                    hs_ref[s * pairs + j, g * N_SSD:(g + 1) * N_SSD, :] = h0_ref[s, j]

    _causal_conv(xbc_ref, hist_ref, cw_ref, cb_ref, cbuf_ref, cnew_ref, xc_ref, nseq, lseq, first, last)
    xbc = _silu(xc_ref[...])
    xs = xbc[:, :D_SSD]
    bm = xbc[:, D_SSD:D_SSD + G_SSD * N_SSD]
    cm = xbc[:, D_SSD + G_SSD * N_SSD:]

    lane = lax.broadcasted_iota(jnp.int32, (1, V7X_LANES), 1)
    is_dt = (lane >= DT_LANE0) & (lane < DT_LANE0 + H_SSD)
    dt = _softplus(dt_ref[...] + dtb_ref[...])
    a_neg = jnp.where(is_dt, -jnp.exp(alog_ref[...]), 0.0)
    sums = _sel_rows(sel_ref[0:2 * rows, :], dt * a_neg)
    cs = sums[:rows]
    cs_tot = sums[rows:]
    expanded = _sel_lanes(jnp.concatenate([dt, jnp.exp(cs), jnp.exp(cs_tot - cs)], axis=0), exp_ref[...])
    dt_x = expanded[:rows]
    dec_in = expanded[rows:2 * rows]
    dec_out = expanded[2 * rows:]
    xdt = xs * dt_x
    cs_t = cs.T

    row = lax.broadcasted_iota(jnp.int32, (rows, rows), 0)
    col = lax.broadcasted_iota(jnp.int32, (rows, rows), 1)
    visible = jnp.where(row >= col, row ^ col, lseq) < lseq
    lane_lo = lane < P_SSD
    cmb = cm.astype(BF16)
    bmb = bm.astype(BF16)
    gram = []
    for g in range(G_SSD):
        in_group = (lane >= g * N_SSD) & (lane < (g + 1) * N_SSD)
        gram.append(_dot_nt(jnp.where(in_group, cm, 0.0).astype(BF16), bmb))
    sub = lax.broadcasted_iota(jnp.int32, (V7X_LANES, 1), 0)

    for j in range(pairs):
        g = j // pairs_per_group
        ps = slice(j * V7X_LANES, (j + 1) * V7X_LANES)
        x_pair = xdt[:, ps]
        y = None
        for half, keep_lanes in ((0, lane_lo), (1, ~lane_lo)):
            hl = DT_LANE0 + 2 * j + half
            seg = cs[:, hl:hl + 1] - cs_t[hl:hl + 1, :]
            decay = jnp.exp(jnp.where(visible, seg, -jnp.inf))
            part = _dot((gram[g] * decay).astype(BF16), jnp.where(keep_lanes, x_pair, 0.0).astype(BF16))
            y = part if y is None else y + part
        x_out = x_pair * dec_out[:, ps]
        in_rows = (sub >= g * N_SSD) & (sub < (g + 1) * N_SSD)
        carry = []
        for s in range(nseq):
            rs = slice(s * lseq, (s + 1) * lseq)
            idx = s * pairs + j
            state = hs_ref[idx]
            carry.append(_dot(cmb[rs], state.astype(BF16)))
            upd = _dot_tn(bmb[rs], x_out[rs].astype(BF16))
            total = dec_in[(s + 1) * lseq - 1:(s + 1) * lseq, ps]
            hs_ref[idx] = state * total + jnp.where(in_rows, upd, 0.0)
        inter = carry[0] if nseq == 1 else jnp.concatenate(carry, axis=0)
        y = y + dec_in[:, ps] * inter + dexp_ref[:, ps] * xs[:, ps]
        xc_ref[:, ps] = y * _silu(z_ref[:, ps])

    gw = D_SSD // G_SSD
    for g in range(G_SSD):
        gs = slice(g * gw, (g + 1) * gw)
        y_ref[:, gs] = _rmsnorm(xc_ref[:, gs], ng_ref[:, gs])

    @pl.when(last)
    def _store_state():
        for s in range(nseq):
            for j in range(pairs):
                g = j // pairs_per_group
                hn_ref[s, j] = hs_ref[s * pairs + j, g * N_SSD:(g + 1) * N_SSD, :]


def _ssd(z, xbc, dtblk, params, state, nb, seqlen, row0):
    lseq = min(seqlen, ROW_BLOCK)
    nseq = ROW_BLOCK // lseq
    nblk_b, nblk_l = nb // nseq, seqlen // lseq
    blk0 = row0 // ROW_BLOCK
    rowmap = lambda b, l: (blk0 + b * nblk_l + l, 0)
    fixed = lambda b, l: (0, 0)
    pairs = H_SSD // 2
    consts = [_head_expand_const(), _seq_consts(ROW_BLOCK, lseq)]
    acts = [z, xbc, dtblk]
    in_specs = [pl.BlockSpec((ROW_BLOCK, a.shape[1]), rowmap) for a in acts]
    in_specs += [pl.BlockSpec(p.shape, fixed) for p in list(params) + consts]
    args = acts + list(params) + consts
    cb_block = (nseq, CONV_W - 1, SSD_CONV_DIM)
    h_block = (nseq, pairs, N_SSD, 2 * P_SSD)
    if state is not None:
        in_specs += [pl.BlockSpec(cb_block, lambda b, l: (b, 0, 0)),
                     pl.BlockSpec(h_block, lambda b, l: (b, 0, 0, 0))]
        args += list(state)
    return pl.pallas_call(
        functools.partial(_ssd_kernel, nseq, lseq, state is not None),
        grid=(nblk_b, nblk_l),
        in_specs=in_specs,
        out_specs=[pl.BlockSpec((ROW_BLOCK, D_SSD), lambda b, l: (b * nblk_l + l, 0)),
                   pl.BlockSpec(cb_block, lambda b, l: (b, 0, 0)),
                   pl.BlockSpec(h_block, lambda b, l: (b, 0, 0, 0))],
        out_shape=[jax.ShapeDtypeStruct((nb * seqlen, D_SSD), F32),
                   jax.ShapeDtypeStruct((nb,) + cb_block[1:], F32),
                   jax.ShapeDtypeStruct((nb,) + h_block[1:], F32)],
        scratch_shapes=[pltpu.VMEM((nseq, CONV_PAD + lseq, SSD_CONV_DIM), F32),
                        pltpu.VMEM((ROW_BLOCK, SSD_CONV_DIM), F32),
                        pltpu.VMEM((nseq * pairs, V7X_LANES, V7X_LANES), F32)],
        compiler_params=_cparams(("parallel", "arbitrary")),
        name="ssd",
    )(*args)


def _rglru_kernel(nseq, lseq, has_state, *refs):
    rows = nseq * lseq
    xr_ref, gate_ref, cw_ref, cb_ref, wa_ref, ba_ref, wi_ref, bi_ref, lam_ref = refs[:9]
    rest = refs[9:]
    if has_state:
        cbuf_ref, h0_ref, y_ref, cnew_ref, hn_ref, hist_ref, xc_ref, hc_ref = rest
    else:
        y_ref, cnew_ref, hn_ref, hist_ref, xc_ref, hc_ref = rest
        cbuf_ref = h0_ref = None
    step = pl.program_id(1)
    first = step == 0
    last = step == pl.num_programs(1) - 1

    @pl.when(first)
    def _load_state():
        for s in range(nseq):
            hc_ref[s] = h0_ref[s] if has_state else jnp.zeros((1, D_RG), F32)

    _causal_conv(xr_ref, hist_ref, cw_ref, cb_ref, cbuf_ref, cnew_ref, xc_ref, nseq, lseq, first, last)
    xc = xc_ref[...]
    xcb = xc.astype(BF16)
    r = _sigmoid(_dot(xcb, wa_ref[...]) + ba_ref[...])
    ig = _sigmoid(_dot(xcb, wi_ref[...]) + bi_ref[...])
    log_a = -RG_C * r * _softplus(-lam_ref[...])
    a = jnp.exp(log_a)
    u = jnp.sqrt(-_expm1(2.0 * log_a)) * (ig * xc)
    pos = lax.broadcasted_iota(jnp.int32, (rows, 1), 0) & (lseq - 1)
    d = 1
    while d < lseq:
        ok = pos >= d
        u = jnp.where(ok, a * pltpu.roll(u, d, 0) + u, u)
        a = jnp.where(ok, a * pltpu.roll(a, d, 0), a)
        d *= 2
    if nseq == 1:
        h_prev = hc_ref[0]
    else:
        h_prev = jnp.concatenate([jnp.broadcast_to(hc_ref[s], (lseq, D_RG)) for s in range(nseq)], axis=0)
    h = a * h_prev + u
    y_ref[...] = h * _gelu_tanh(gate_ref[...])
    for s in range(nseq):
        hc_ref[s] = h[(s + 1) * lseq - 1:(s + 1) * lseq, :]

    @pl.when(last)
    def _store_state():
        for s in range(nseq):
            hn_ref[s] = hc_ref[s]


def _rglru(xr, gate, params, state, nb, seqlen, row0):
    lseq = min(seqlen, ROW_BLOCK)
    nseq = ROW_BLOCK // lseq
    nblk_b, nblk_l = nb // nseq, seqlen // lseq
    blk0 = row0 // ROW_BLOCK
    rowmap = lambda b, l: (blk0 + b * nblk_l + l, 0)
    fixed = lambda b, l: (0, 0)
    acts = [xr, gate]
    in_specs = [pl.BlockSpec((ROW_BLOCK, D_RG), rowmap) for _ in acts]
    in_specs += [pl.BlockSpec(p.shape, fixed) for p in params]
    args = acts + list(params)
    cb_block = (nseq, CONV_W - 1, D_RG)
    h_block = (nseq, 1, D_RG)
    if state is not None:
        in_specs += [pl.BlockSpec(cb_block, lambda b, l: (b, 0, 0)),
                     pl.BlockSpec(h_block, lambda b, l: (b, 0, 0))]
        args += list(state)
    return pl.pallas_call(
        functools.partial(_rglru_kernel, nseq, lseq, state is not None),
        grid=(nblk_b, nblk_l),
        in_specs=in_specs,
        out_specs=[pl.BlockSpec((ROW_BLOCK, D_RG), lambda b, l: (b * nblk_l + l, 0)),
                   pl.BlockSpec(cb_block, lambda b, l: (b, 0, 0)),
                   pl.BlockSpec(h_block, lambda b, l: (b, 0, 0))],
        out_shape=[jax.ShapeDtypeStruct((nb * seqlen, D_RG), F32),
                   jax.ShapeDtypeStruct((nb,) + cb_block[1:], F32),
                   jax.ShapeDtypeStruct((nb,) + h_block[1:], F32)],
        scratch_shapes=[pltpu.VMEM((nseq, CONV_PAD + lseq, D_RG), F32),
                        pltpu.VMEM((ROW_BLOCK, D_RG), F32),
                        pltpu.VMEM((nseq, 1, D_RG), F32)],
        compiler_params=_cparams(("parallel", "arbitrary")),
        name="rglru",
    )(*args)


def _route(x, wr_ref, br_ref, comb_t_ref):
    logits = lax.dot_general(wr_ref[...], x, (((1,), (1,)), ((), ())),
                             precision=lax.Precision.HIGHEST, preferred_element_type=F32) + br_ref[...]
    gl = [logits[i:i + 1, :] for i in range(N_GROUPS)]
    gmax = functools.reduce(jnp.maximum, gl)
    gidx = jnp.full_like(gmax, N_GROUPS - 1, dtype=jnp.int32)
    for i in range(N_GROUPS - 2, -1, -1):
        gidx = jnp.where(gl[i] == gmax, i, gidx)
    p_g = 1.0 / functools.reduce(lambda a, b: a + b, [jnp.exp(v - gmax) for v in gl])
    el = []
    for e in range(E_PER_GROUP):
        v = logits[N_GROUPS + e:N_GROUPS + e + 1, :]
        for g in range(1, N_GROUPS):
            r0 = N_GROUPS + g * E_PER_GROUP + e
            v = jnp.where(gidx == g, logits[r0:r0 + 1, :], v)
        el.append(v)
    v1 = functools.reduce(jnp.maximum, el)
    i1 = jnp.full_like(gidx, E_PER_GROUP - 1)
    for e in range(E_PER_GROUP - 2, -1, -1):
        i1 = jnp.where(el[e] == v1, e, i1)
    rest = [jnp.where(i1 == e, -jnp.inf, el[e]) for e in range(E_PER_GROUP)]
    v2 = functools.reduce(jnp.maximum, rest)
    i2 = jnp.full_like(gidx, E_PER_GROUP - 1)
    for e in range(E_PER_GROUP - 2, -1, -1):
        i2 = jnp.where(rest[e] == v2, e, i2)
    ex = jnp.exp(v2 - v1)
    w1 = 1.0 / (1.0 + ex)
    w2 = ex / (1.0 + ex)
    within = [jnp.where(i1 == e, w1, 0.0) + jnp.where(i2 == e, w2, 0.0) for e in range(E_PER_GROUP)]
    for g in range(N_GROUPS):
        pg = jnp.where(gidx == g, p_g, 0.0)
        for e in range(E_PER_GROUP):
            r0 = g * E_PER_GROUP + e
            comb_t_ref[r0:r0 + 1, :] = pg * within[e]


def _moe_dense_kernel(x_ref, wr_ref, br_ref, wg_ref, wu_ref, wd_ref, g_ref, b_ref, o_ref,
                      xb_ref, comb_t_ref, comb_ref, acc_ref):
    e = pl.program_id(1)

    @pl.when(e == 0)
    def _start():
        x = x_ref[...]
        xb_ref[...] = x.astype(BF16)
        comb_t_ref[...] = jnp.zeros_like(comb_t_ref)
        _route(x, wr_ref, br_ref, comb_t_ref)
        comb_ref[...] = comb_t_ref[...].T
        acc_ref[...] = jnp.zeros_like(acc_ref)

    xb = xb_ref[...]
    hidden = _silu(_dot(xb, wg_ref[...].astype(BF16))) * _dot(xb, wu_ref[...].astype(BF16))
    lane = lax.broadcasted_iota(jnp.int32, (1, V7X_LANES), 1)
    weight = jnp.sum(jnp.where(lane == e, comb_ref[...], 0.0), axis=1, keepdims=True)
    acc_ref[...] += _dot((hidden * weight).astype(BF16), wd_ref[...].astype(BF16))

    @pl.when(e == N_EXPERTS - 1)
    def _finish():
        o_ref[...] = _layernorm(DN_ALPHA * x_ref[...] + acc_ref[...], g_ref[...], b_ref[...])


def _moe_dense(x, wr, br, w_gate, w_up, w_down, g, b, layer):
    t = x.shape[0]
    tm = _row_tile(t, 1024)
    row = lambda i, e: (i, 0)
    fixed = lambda i, e: (0, 0)
    wmap = lambda i, e: (layer, e // E_PER_GROUP, e % E_PER_GROUP, 0, 0)
    sq = pl.Squeezed()
    return pl.pallas_call(
        _moe_dense_kernel,
        grid=(t // tm, N_EXPERTS),
        in_specs=[pl.BlockSpec((tm, D_MODEL), row), pl.BlockSpec(wr.shape, fixed), pl.BlockSpec(br.shape, fixed),
                  pl.BlockSpec((sq, sq, sq, D_MODEL, D_EXPERT), wmap),
                  pl.BlockSpec((sq, sq, sq, D_MODEL, D_EXPERT), wmap),
                  pl.BlockSpec((sq, sq, sq, D_EXPERT, D_MODEL), wmap),
                  pl.BlockSpec((1, D_MODEL), fixed), pl.BlockSpec((1, D_MODEL), fixed)],
        out_specs=pl.BlockSpec((tm, D_MODEL), row),
        out_shape=jax.ShapeDtypeStruct((t, D_MODEL), F32),
        scratch_shapes=[pltpu.VMEM((tm, D_MODEL), BF16), pltpu.VMEM((V7X_LANES, tm), F32),
                        pltpu.VMEM((tm, V7X_LANES), F32), pltpu.VMEM((tm, D_MODEL), F32)],
        compiler_params=_cparams(("parallel", "arbitrary")),
        name="moe_dense",
    )(x, wr, br, w_gate, w_up, w_down, g, b)


def _pad_heads(w, heads, dk):
    lead = w.shape[:-1]
    w = w.reshape(lead + (heads, dk))
    w = jnp.pad(w, [(0, 0)] * len(lead) + [(0, 0), (0, V7X_LANES - dk)])
    return w.reshape(lead + (heads * V7X_LANES,))


def _pad_lanes(w, left, total=V7X_LANES):
    return jnp.pad(w, [(0, 0)] * (w.ndim - 1) + [(left, total - left - w.shape[-1])])


def _block_diag(w):
    h, bd, _ = w.shape
    eye = jnp.eye(h, dtype=w.dtype)
    return (eye[:, None, :, None] * w[:, :, None, :]).reshape(h * bd, h * bd)


AB_WIDTHS = (512, 512, 512, 512, 512, SSD_CONV_DIM, V7X_LANES)
CD_WIDTHS = (512, 512, 512, 512, D_RG, D_RG)


def _prep_w_in_ab(w):
    hk = N_HEADS_LIN * DK_LIN
    hv = N_HEADS_LIN * DV_LIN
    o = np.cumsum((0, hk, hk, hv, hv, GLA_RANK, D_SSD, SSD_CONV_DIM, H_SSD))
    q, k, v, g, lr, z, xbc, dt = (w[:, o[i]:o[i + 1]] for i in range(8))
    lrdt = _pad_lanes(jnp.concatenate([lr, dt], axis=1), 0)
    cols = [_pad_heads(q, N_HEADS_LIN, DK_LIN), _pad_heads(k, N_HEADS_LIN, DK_LIN), v, g, z, xbc, lrdt]
    return jnp.concatenate(cols, axis=1).astype(BF16)


def _prep_w_in_cd(w):
    hk = N_HEADS_LIN * DK_LIN
    hv = N_HEADS_LIN * DV_LIN
    o = np.cumsum((0, hk, hk, hv, hv, D_RG, D_RG))
    q, f, i, g, x, gate = (w[:, o[j]:o[j + 1]] for j in range(6))
    cols = [_pad_heads(q, N_HEADS_LIN, DK_LIN), _pad_heads(f, N_HEADS_LIN, DK_LIN), i, g, x, gate]
    return jnp.concatenate(cols, axis=1).astype(BF16)


def kernel(x_prompt, x_sample, state_gla, state_ssd, state_ssd_conv, state_hgrn, state_rglru, state_rglru_conv,
           w_in_ab, gla_w_gk2, gla_b_gk, gla_norm_g, ssd_conv_w, ssd_conv_b, ssd_dt_bias, ssd_a_log, ssd_d,
           ssd_norm_g, w_out_ab, w_in_cd, hgrn_lb_logits, hgrn_norm_g, rg_conv_w, rg_conv_b, rg_w_a, rg_b_a,
           rg_w_i, rg_b_i, rg_lambda, w_out_cd, ln_mix_g, ln_mix_b, ln_ffn_g, ln_ffn_b, moe_w_group,
           moe_b_group, moe_w_router, moe_b_router, moe_w_gate, moe_w_up, moe_w_down):
    nb_p, len_p, _ = x_prompt.shape
    nb_s, len_s, _ = x_sample.shape
    t_p = nb_p * len_p
    x = jnp.concatenate([x_prompt.reshape(t_p, D_MODEL), x_sample.reshape(nb_s * len_s, D_MODEL)], axis=0)
    groups = ((nb_p, len_p, 0), (nb_s, len_s, t_p))

    lb_all = jnp.cumsum(jax.nn.softmax(hgrn_lb_logits.astype(F32), axis=0), axis=0)
    lb_all = lb_all - lb_all[:1]
    pairs = H_SSD // 2

    def ssd_state_in(h):
        b = h.shape[0]
        return h.reshape(b, pairs, 2, N_SSD, P_SSD).transpose(0, 1, 3, 2, 4).reshape(b, pairs, N_SSD, 2 * P_SSD)

    def ssd_state_out(h):
        b = h.shape[0]
        return h.reshape(b, pairs, N_SSD, 2, P_SSD).transpose(0, 1, 3, 2, 4).reshape(b, H_SSD, N_SSD, P_SSD)

    outs = {k: ([], []) for k in ("gla", "ssd", "ssdc", "hgrn", "rg", "rgc")}
    for layer in range(DEPTH):
        i = layer // 2
        if layer % 2 == 0:
            q, k, v, g, z, xbc, lrdt = _inproj(x, _prep_w_in_ab(w_in_ab[i]), AB_WIDTHS)
            gla_params = [_pad_heads(_pad_lanes(gla_w_gk2[i].T, 0).T, N_HEADS_LIN, DK_LIN).astype(BF16),
                          _pad_heads(gla_b_gk[i][None, :], N_HEADS_LIN, DK_LIN),
                          gla_norm_g[i][None, :]]
            ssd_params = [ssd_conv_w[i], ssd_conv_b[i][None, :],
                          _pad_lanes(ssd_dt_bias[i][None, :], DT_LANE0), _pad_lanes(ssd_a_log[i][None, :], DT_LANE0),
                          jnp.repeat(ssd_d[i], P_SSD)[None, :], ssd_norm_g[i][None, :]]
            oa, ob = [], []
            for gi, (nb, sl, row0) in enumerate(groups):
                fresh = gi == 0
                o, s_new = _lin_attn("gla", [q, k, v, g, lrdt], gla_params,
                                     None if fresh else state_gla[i], nb, sl, row0)
                st = None if fresh else (state_ssd_conv[i], ssd_state_in(state_ssd[i]))
                y, c_new, h_new = _ssd(z, xbc, lrdt, ssd_params, st, nb, sl, row0)
                oa.append(o)
                ob.append(y)
                outs["gla"][gi].append(s_new)
                outs["ssd"][gi].append(ssd_state_out(h_new))
                outs["ssdc"][gi].append(c_new)
            w_out = w_out_ab[i]
        else:
            q, f, v, g, xr, gate = _inproj(x, _prep_w_in_cd(w_in_cd[i]), CD_WIDTHS)
            hg_params = [_pad_heads(lb_all[i][None, :], N_HEADS_LIN, DK_LIN), hgrn_norm_g[i][None, :]]
            rg_params = [rg_conv_w[i], rg_conv_b[i][None, :],
                         _block_diag(rg_w_a[i]).astype(BF16), rg_b_a[i][None, :],
                         _block_diag(rg_w_i[i]).astype(BF16), rg_b_i[i][None, :], rg_lambda[i][None, :]]
            oa, ob = [], []
            for gi, (nb, sl, row0) in enumerate(groups):
                fresh = gi == 0
                o, s_new = _lin_attn("hgrn", [q, f, v, g], hg_params,
                                     None if fresh else state_hgrn[i], nb, sl, row0)
                st = None if fresh else (state_rglru_conv[i], state_rglru[i][:, None, :])
                y, c_new, h_new = _rglru(xr, gate, rg_params, st, nb, sl, row0)
                oa.append(o)
                ob.append(y)
                outs["hgrn"][gi].append(s_new)
                outs["rg"][gi].append(h_new[:, 0, :])
                outs["rgc"][gi].append(c_new)
            w_out = w_out_cd[i]
        x = _outproj_ln(x, jnp.concatenate(oa, axis=0), jnp.concatenate(ob, axis=0), w_out.astype(BF16),
                        ln_mix_g[layer][None, :], ln_mix_b[layer][None, :])
        wr = jnp.concatenate([moe_w_group[layer]] + [moe_w_router[layer, gg] for gg in range(N_GROUPS)], axis=1)
        br = jnp.concatenate([moe_b_group[layer], moe_b_router[layer].reshape(-1)])
        nr = wr.shape[1]
        wr = jnp.pad(wr.T, ((0, V7X_LANES - nr), (0, 0)))
        br = jnp.pad(br, (0, V7X_LANES - nr))[:, None]
        x = _moe_dense(x, wr, br, moe_w_gate, moe_w_up, moe_w_down,
                       ln_ffn_g[layer][None, :], ln_ffn_b[layer][None, :], layer)

    y_prompt = x[:t_p].reshape(nb_p, len_p, D_MODEL)
    y_sample = x[t_p:].reshape(nb_s, len_s, D_MODEL)
    res = [y_prompt, y_sample]
    for gi in range(2):
        res += [jnp.stack(outs[k][gi]) for k in ("gla", "ssd", "ssdc", "hgrn", "rg", "rgc")]
    return tuple(res)
```
